```python
import jax, jax.numpy as jnp
from jax import lax
import numpy as np

D_MODEL = 1024
BATCH = 8
SEQ = 8192
DEPTH = 1

PLE_DIM = 256
MIX_WIDTH = D_MODEL
ATTN_WIDTH = MIX_WIDTH // 2
CONV_WIDTH = MIX_WIDTH - ATTN_WIDTH
HEAD_DIM = 64
N_ATTN_HEADS = ATTN_WIDTH // HEAD_DIM
CONV_KERNEL = 31
FFN_CONV_KERNEL = 3
D_FF = 2816
Q_BLOCK = 128
EPS = 1e-6
IN_COLS = 3 * ATTN_WIDTH + 2 * CONV_WIDTH

kernel_name = "hymba_stickbreak_conformer_convglu_ple"


def rms_norm(x, g):
    xf = x.astype(jnp.float32)
    y = xf * lax.rsqrt(jnp.mean(xf * xf, axis=-1, keepdims=True) + EPS)
    return (y * g.astype(jnp.float32)).astype(x.dtype)


def layer_norm(x, g, b):
    xf = x.astype(jnp.float32)
    mu = jnp.mean(xf, axis=-1, keepdims=True)
    xc = xf - mu
    y = xc * lax.rsqrt(jnp.mean(xc * xc, axis=-1, keepdims=True) + EPS)
    return (y * g.astype(jnp.float32) + b.astype(jnp.float32)).astype(x.dtype)


def causal_dwconv(x, w, b):
    k_width, chans = w.shape
    y = lax.conv_general_dilated(
        x, w[:, None, :].astype(x.dtype), window_strides=(1,), padding=[(k_width - 1, 0)],
        dimension_numbers=("NWC", "WIO", "NWC"), feature_group_count=chans)
    return y + b.astype(x.dtype)


def stick_breaking_attention(q, k, v):
    seq = q.shape[2]
    scale = HEAD_DIM ** -0.5
    outs = []
    for blk in range(seq // Q_BLOCK):
        q0 = blk * Q_BLOCK
        q1 = q0 + Q_BLOCK
        qb = q[:, :, q0:q1]
        kb = k[:, :, :q1]
        vb = v[:, :, :q1]
        z = jnp.einsum("bhqd,bhkd->bhqk", qb, kb).astype(jnp.float32) * scale
        causal = jnp.arange(q1)[None, :] < jnp.arange(q0, q1)[:, None]
        log_beta = jax.nn.log_sigmoid(z)
        log_one_minus = jnp.where(causal, log_beta - z, 0.0)
        between = lax.cumsum(log_one_minus, axis=3, reverse=True) - log_one_minus
        weights = jnp.where(causal, jnp.exp(log_beta + between), 0.0)
        outs.append(jnp.einsum("bhqk,bhkd->bhqd", weights.astype(v.dtype), vb))
    return jnp.concatenate(outs, axis=2)


def setup_inputs(seed: int = 0) -> dict:
    key = jax.random.key(seed)
    ks = jax.random.split(key, 20)
    f32 = jnp.float32
    nrm = lambda k, shape, s: jax.random.normal(k, shape, f32) * s
    gain = lambda k, shape: 1.0 + 0.05 * jax.random.normal(k, shape, f32)
    return {
        "x": jax.random.normal(ks[0], (BATCH, SEQ, D_MODEL), f32),
        "p": jax.random.normal(ks[1], (DEPTH, BATCH, SEQ, PLE_DIM), f32),
        "g_mix": gain(ks[2], (DEPTH, D_MODEL)),
        "w_in": nrm(ks[3], (DEPTH, D_MODEL, IN_COLS), D_MODEL ** -0.5),
        "q_gain": gain(ks[4], (DEPTH, HEAD_DIM)),
        "k_gain": gain(ks[5], (DEPTH, HEAD_DIM)),
        "dw_w": nrm(ks[6], (DEPTH, CONV_KERNEL, CONV_WIDTH), CONV_KERNEL ** -0.5),
        "dw_b": nrm(ks[7], (DEPTH, CONV_WIDTH), 0.02),
        "conv_ln_g": gain(ks[8], (DEPTH, CONV_WIDTH)),
        "conv_ln_b": nrm(ks[9], (DEPTH, CONV_WIDTH), 0.02),
        "w_out": nrm(ks[10], (DEPTH, MIX_WIDTH, D_MODEL), MIX_WIDTH ** -0.5),
        "g_ffn": gain(ks[11], (DEPTH, D_MODEL)),
        "w_up": nrm(ks[12], (DEPTH, D_MODEL, 2 * D_FF), D_MODEL ** -0.5),
        "ffn_conv_w": nrm(ks[13], (DEPTH, FFN_CONV_KERNEL, D_FF), FFN_CONV_KERNEL ** -0.5),
        "ffn_conv_b": nrm(ks[14], (DEPTH, D_FF), 0.02),
        "w_down": nrm(ks[15], (DEPTH, D_FF, D_MODEL), D_FF ** -0.5),
        "g_ple": gain(ks[16], (DEPTH, D_MODEL)),
        "w_ple_gate": nrm(ks[17], (DEPTH, D_MODEL, D_MODEL), D_MODEL ** -0.5),
        "w_ple_proj": nrm(ks[18], (DEPTH, PLE_DIM, D_MODEL), PLE_DIM ** -0.5),
    }


def reference(x, p, g_mix, w_in, q_gain, k_gain, dw_w, dw_b, conv_ln_g, conv_ln_b, w_out,
              g_ffn, w_up, ffn_conv_w, ffn_conv_b, w_down, g_ple, w_ple_gate, w_ple_proj):
    bsz, seq, _ = x.shape
    h = x
    for i in range(DEPTH):
        u = rms_norm(h, g_mix[i])
        proj = u @ w_in[i]
        q, k, v, ca, cg = jnp.split(
            proj, np.cumsum([ATTN_WIDTH] * 3 + [CONV_WIDTH]).tolist(), axis=-1)
        heads = lambda t: t.reshape(bsz, seq, N_ATTN_HEADS, HEAD_DIM).transpose(0, 2, 1, 3)
        qh = rms_norm(heads(q), q_gain[i])
        kh = rms_norm(heads(k), k_gain[i])
        attn = stick_breaking_attention(qh, kh, heads(v))
        attn = attn.transpose(0, 2, 1, 3).reshape(bsz, seq, ATTN_WIDTH)
        c = ca * jax.nn.sigmoid(cg)
        c = causal_dwconv(c, dw_w[i], dw_b[i])
        c = jax.nn.silu(layer_norm(c, conv_ln_g[i], conv_ln_b[i]))
        h = h + jnp.concatenate([attn, c], axis=-1) @ w_out[i]
        u = rms_norm(h, g_ffn[i])
        gate, val = jnp.split(u @ w_up[i], 2, axis=-1)
        gate = causal_dwconv(gate, ffn_conv_w[i], ffn_conv_b[i])
        h = h + (jax.nn.gelu(gate, approximate=False) * val) @ w_down[i]
        ple_gate = jax.nn.sigmoid(rms_norm(h, g_ple[i]) @ w_ple_gate[i])
        h = h + ple_gate * (p[i] @ w_ple_proj[i])
    return h
```

```python
import functools

import jax
import jax.numpy as jnp
from jax import lax
from jax.experimental import pallas as pl
from jax.experimental.pallas import tpu as pltpu

HEAD_DIM = 64
CONV_KERNEL = 31
FFN_CONV_KERNEL = 3
EPS = 1e-6

LANES = 128
SUBLANES = 8
VMEM_LIMIT_BYTES = 56 * 1024 * 1024

ROW_TILE_IN = 512
ROW_TILE_OUT = 256
ATTN_BLOCK = 256
CONV_HALO = 32
CONV_ROW_CHUNK = 64

BF16 = jnp.bfloat16
F32 = jnp.float32


def _const_spec(shape):
    return pl.BlockSpec(shape, lambda *_: (0,) * len(shape), pipeline_mode=pl.Buffered(1))


def _rms_scale(x):
    return lax.rsqrt(jnp.mean(x * x, axis=-1, keepdims=True) + EPS)


def _mix_in_kernel(x_ref, g_ref, w_ref, seg_ref, qg_ref, kg_ref, dww_ref, dwb_ref, lng_ref, lnb_ref,
                   q_ref, k_ref, v_ref, c_ref, cbuf_ref, *, attn_w, conv_w):
    tm = x_ref.shape[0]
    x = x_ref[...]
    u = (x * _rms_scale(x) * g_ref[...]).astype(BF16)

    def head_norm(cols, gain_ref, out_ref):
        t = jnp.dot(u, w_ref[:, cols:cols + attn_w], preferred_element_type=F32)
        ms = jnp.dot((t * t).astype(BF16), seg_ref[...], preferred_element_type=F32)
        out_ref[...] = (t * lax.rsqrt(ms + EPS) * gain_ref[...]).astype(BF16)

    head_norm(0, qg_ref, q_ref)
    head_norm(attn_w, kg_ref, k_ref)
    v_ref[...] = jnp.dot(u, w_ref[:, 2 * attn_w:3 * attn_w], preferred_element_type=F32).astype(BF16)

    ca = jnp.dot(u, w_ref[:, 3 * attn_w:3 * attn_w + conv_w], preferred_element_type=F32)
    cg = jnp.dot(u, w_ref[:, 3 * attn_w + conv_w:3 * attn_w + 2 * conv_w], preferred_element_type=F32)

    @pl.when(pl.program_id(1) == 0)
    def _():
        cbuf_ref[0:CONV_HALO, :] = jnp.zeros((CONV_HALO, conv_w), F32)

    cbuf_ref[CONV_HALO:CONV_HALO + tm, :] = ca * jax.nn.sigmoid(cg)

    first_tap = CONV_HALO - (CONV_KERNEL - 1)
    for r0 in range(0, tm, CONV_ROW_CHUNK):
        acc = jnp.broadcast_to(dwb_ref[...], (CONV_ROW_CHUNK, conv_w))
        for tap in range(CONV_KERNEL):
            acc = acc + dww_ref[tap:tap + 1, :] * cbuf_ref[r0 + first_tap + tap:r0 + first_tap + tap + CONV_ROW_CHUNK, :]
        mu = jnp.mean(acc, axis=-1, keepdims=True)
        xc = acc - mu
        y = xc * lax.rsqrt(jnp.mean(xc * xc, axis=-1, keepdims=True) + EPS)
        y = y * lng_ref[...] + lnb_ref[...]
        c_ref[r0:r0 + CONV_ROW_CHUNK, :] = (y * jax.nn.sigmoid(y)).astype(BF16)

    cbuf_ref[0:CONV_HALO, :] = cbuf_ref[tm:tm + CONV_HALO, :]


def _mix_in(x, g_mix, w_in, seg, q_gain, k_gain, dw_w, dw_b, ln_g, ln_b, *, attn_w, conv_w):
    bsz, seq, d = x.shape
    tm = min(ROW_TILE_IN, seq)
    assert seq % tm == 0 and tm % CONV_ROW_CHUNK == 0 and tm >= CONV_HALO
    row = lambda width: pl.BlockSpec((None, tm, width), lambda b, s: (b, s, 0))
    out_sds = jax.ShapeDtypeStruct((bsz, seq, attn_w), BF16)
    return pl.pallas_call(
        functools.partial(_mix_in_kernel, attn_w=attn_w, conv_w=conv_w),
        grid=(bsz, seq // tm),
        in_specs=[row(d), _const_spec(g_mix.shape), _const_spec(w_in.shape), _const_spec(seg.shape),
                  _const_spec(q_gain.shape), _const_spec(k_gain.shape), _const_spec(dw_w.shape),
                  _const_spec(dw_b.shape), _const_spec(ln_g.shape), _const_spec(ln_b.shape)],
        out_specs=[row(attn_w), row(attn_w), row(attn_w), row(conv_w)],
        out_shape=[out_sds, out_sds, out_sds, jax.ShapeDtypeStruct((bsz, seq, conv_w), BF16)],
        scratch_shapes=[pltpu.VMEM((tm + CONV_HALO, conv_w), F32)],
        compiler_params=pltpu.CompilerParams(
            dimension_semantics=("arbitrary", "arbitrary"), vmem_limit_bytes=VMEM_LIMIT_BYTES),
        name="mix_in",
    )(x, g_mix, w_in, seg, q_gain, k_gain, dw_w, dw_b, ln_g, ln_b)


def _stickbreak_kernel(q_ref, k_ref, v_ref, tri_ref, o_ref):
    bq = q_ref.shape[0]
    blk = pl.program_id(2)

    q = q_ref[...]
    lane = lax.broadcasted_iota(jnp.int32, (bq, LANES), 1)
    zero = jnp.zeros_like(q)
    q2 = jnp.concatenate([jnp.where(lane < HEAD_DIM, q, zero), jnp.where(lane < HEAD_DIM, zero, q)], axis=0)
    tri = tri_ref[...]

    def block(kv_blk, run, acc, diagonal):
        start = pl.multiple_of(kv_blk * bq, bq)
        kb = k_ref[pl.ds(start, bq), :]
        vb = v_ref[pl.ds(start, bq), :]
        z = lax.dot_general(q2, kb, (((1,), (1,)), ((), ())), preferred_element_type=F32)
        soft = jnp.log1p(jnp.exp(-jnp.abs(z)))
        log_beta = jnp.minimum(z, 0.0) - soft
        log_rest = -jnp.maximum(z, 0.0) - soft
        if diagonal:
            row = lax.broadcasted_iota(jnp.int32, (2 * bq, bq), 0)
            col = lax.broadcasted_iota(jnp.int32, (2 * bq, bq), 1)
            causal = col < jnp.where(row >= bq, row - bq, row)
            log_rest = jnp.where(causal, log_rest, 0.0)
        hi = log_rest.astype(BF16)
        lo = (log_rest - hi.astype(F32)).astype(BF16)
        between = (jnp.dot(hi, tri, preferred_element_type=F32)
                   + jnp.dot(lo, tri, preferred_element_type=F32))
        w = jnp.exp(log_beta + between + run)
        if diagonal:
            w = jnp.where(causal, w, 0.0)
        acc = acc + jnp.dot(w.astype(BF16), vb, preferred_element_type=F32)
        run = run + jnp.sum(log_rest, axis=-1, keepdims=True)
        return run, acc

    run0 = jnp.zeros((2 * bq, 1), F32)
    acc0 = jnp.zeros((2 * bq, LANES), F32)
    run, acc = block(blk, run0, acc0, diagonal=True)

    def body(i, carry):
        return block(blk - 1 - i, carry[0], carry[1], diagonal=False)

    run, acc = lax.fori_loop(0, blk, body, (run, acc))
    o_ref[...] = jnp.where(lane < HEAD_DIM, acc[:bq], acc[bq:]).astype(o_ref.dtype)


def _stickbreak(q, k, v):
    bsz, seq, width = q.shape
    bq = min(ATTN_BLOCK, seq)
    assert seq % bq == 0 and width % LANES == 0
    tri = (lax.broadcasted_iota(jnp.int32, (bq, bq), 0) > lax.broadcasted_iota(jnp.int32, (bq, bq), 1)).astype(BF16)
    seq_spec = pl.BlockSpec((None, seq, LANES), lambda b, h, i: (b, 0, h))
    blk_spec = pl.BlockSpec((None, bq, LANES), lambda b, h, i: (b, i, h))
    return pl.pallas_call(
        _stickbreak_kernel,
        grid=(bsz, width // LANES, seq // bq),
        in_specs=[blk_spec, seq_spec, seq_spec, _const_spec(tri.shape)],
        out_specs=blk_spec,
        out_shape=jax.ShapeDtypeStruct((bsz, seq, width), BF16),
        compiler_params=pltpu.CompilerParams(
            dimension_semantics=("arbitrary", "arbitrary", "arbitrary"), vmem_limit_bytes=VMEM_LIMIT_BYTES),
        name="stickbreak",
    )(q, k, v, tri)


def _mix_out_ffn_ple_kernel(x_ref, a_ref, c_ref, p_ref, wo_ref, gf_ref, wu_ref, fw_ref, fb_ref, wd_ref,
                            gp_ref, wg_ref, wp_ref, o_ref, gbuf_ref, gcar_ref, *, d_ff, ff_chunk):
    tm = x_ref.shape[0]
    attn_w = a_ref.shape[1]

    h = (x_ref[...]
         + jnp.dot(a_ref[...], wo_ref[0:attn_w, :], preferred_element_type=F32)
         + jnp.dot(c_ref[...], wo_ref[attn_w:, :], preferred_element_type=F32))

    @pl.when(pl.program_id(1) == 0)
    def _():
        gcar_ref[...] = jnp.zeros(gcar_ref.shape, F32)

    u = (h * _rms_scale(h) * gf_ref[...]).astype(BF16)
    ffn = jnp.zeros_like(h)
    for c0 in range(0, d_ff, ff_chunk):
        gate = jnp.dot(u, wu_ref[:, c0:c0 + ff_chunk], preferred_element_type=F32)
        val = jnp.dot(u, wu_ref[:, d_ff + c0:d_ff + c0 + ff_chunk], preferred_element_type=F32)
        gbuf_ref[0:SUBLANES, :] = gcar_ref[:, c0:c0 + ff_chunk]
        gbuf_ref[SUBLANES:SUBLANES + tm, :] = gate
        gcar_ref[:, c0:c0 + ff_chunk] = gate[tm - SUBLANES:, :]
        conv = (fb_ref[:, c0:c0 + ff_chunk]
                + fw_ref[0:1, c0:c0 + ff_chunk] * gbuf_ref[SUBLANES - 2:SUBLANES - 2 + tm, :]
                + fw_ref[1:2, c0:c0 + ff_chunk] * gbuf_ref[SUBLANES - 1:SUBLANES - 1 + tm, :]
                + fw_ref[2:3, c0:c0 + ff_chunk] * gate)
        gelu = 0.5 * conv * (1.0 + lax.erf(conv * (2.0 ** -0.5)))
        act = (gelu * val).astype(BF16)
        ffn = ffn + jnp.dot(act, wd_ref[c0:c0 + ff_chunk, :], preferred_element_type=F32)
    h = h + ffn

    u = (h * _rms_scale(h) * gp_ref[...]).astype(BF16)
    ple_gate = jax.nn.sigmoid(jnp.dot(u, wg_ref[...], preferred_element_type=F32))
    ple = jnp.dot(p_ref[...].astype(BF16), wp_ref[...], preferred_element_type=F32)
    o_ref[...] = h + ple_gate * ple


def _mix_out_ffn_ple(x, attn, conv, p, w_out, g_ffn, w_up, fw, fb, w_down, g_ple, w_gate, w_proj):
    bsz, seq, d = x.shape
    d_ff = w_down.shape[0]
    tm = min(ROW_TILE_OUT, seq)
    ff_chunk = d_ff // 2
    assert seq % tm == 0 and ff_chunk % LANES == 0 and fw.shape[0] == FFN_CONV_KERNEL
    row = lambda width: pl.BlockSpec((None, tm, width), lambda b, s: (b, s, 0))
    consts = (w_out, g_ffn, w_up, fw, fb, w_down, g_ple, w_gate, w_proj)
    return pl.pallas_call(
        functools.partial(_mix_out_ffn_ple_kernel, d_ff=d_ff, ff_chunk=ff_chunk),
        grid=(bsz, seq // tm),
        in_specs=[row(d), row(attn.shape[2]), row(conv.shape[2]), row(p.shape[2])]
                 + [_const_spec(c.shape) for c in consts],
        out_specs=row(d),
        out_shape=jax.ShapeDtypeStruct((bsz, seq, d), F32),
        scratch_shapes=[pltpu.VMEM((tm + SUBLANES, ff_chunk), F32), pltpu.VMEM((SUBLANES, d_ff), F32)],
        compiler_params=pltpu.CompilerParams(
            dimension_semantics=("arbitrary", "arbitrary"), vmem_limit_bytes=VMEM_LIMIT_BYTES),
        name="mix_out_ffn_ple",
    )(x, attn, conv, p, *consts)


def kernel(x, p, g_mix, w_in, q_gain, k_gain, dw_w, dw_b, conv_ln_g, conv_ln_b, w_out,
           g_ffn, w_up, ffn_conv_w, ffn_conv_b, w_down, g_ple, w_ple_gate, w_ple_proj):
    depth = w_in.shape[0]
    conv_w = dw_w.shape[2]
    attn_w = (w_in.shape[2] - 2 * conv_w) // 3
    n_heads = attn_w // HEAD_DIM
    assert dw_w.shape[1] == CONV_KERNEL

    head_of = lax.broadcasted_iota(jnp.int32, (attn_w, attn_w), 0) // HEAD_DIM
    seg = jnp.where(head_of == head_of.T, 1.0 / HEAD_DIM, 0.0).astype(BF16)
    row2d = lambda a: a.reshape(1, -1)

    h = x
    for i in range(depth):
        qg = row2d(jnp.tile(q_gain[i], n_heads)) * (HEAD_DIM ** -0.5)
        kg = row2d(jnp.tile(k_gain[i], n_heads))
        q, k, v, conv = _mix_in(
            h, row2d(g_mix[i]), w_in[i].astype(BF16), seg, qg, kg, dw_w[i], row2d(dw_b[i]),
            row2d(conv_ln_g[i]), row2d(conv_ln_b[i]), attn_w=attn_w, conv_w=conv_w)
        attn = _stickbreak(q, k, v)
        h = _mix_out_ffn_ple(
            h, attn, conv, p[i], w_out[i].astype(BF16), row2d(g_ffn[i]), w_up[i].astype(BF16),
            ffn_conv_w[i], row2d(ffn_conv_b[i]), w_down[i].astype(BF16), row2d(g_ple[i]),
            w_ple_gate[i].astype(BF16), w_ple_proj[i].astype(BF16))
    return h
```

```python
import functools

import jax
import jax.numpy as jnp
from jax import lax
from jax.experimental import pallas as pl
from jax.experimental.pallas import tpu as pltpu

HEAD_DIM = 64
CONV_KERNEL = 31
FFN_CONV_KERNEL = 3
EPS = 1e-6

LANES = 128
SUBLANES = 8
VMEM_LIMIT_BYTES = 56 * 1024 * 1024

ROW_TILE_IN = 512
ROW_TILE_OUT = 256
ATTN_BLOCK = 256
CONV_HALO = 32
CONV_ROW_CHUNK = 64

LOG2_E = 1.4426950408889634
SOFTPLUS_CLAMP = 100.0

BF16 = jnp.bfloat16
F32 = jnp.float32


def _const_spec(shape):
    return pl.BlockSpec(shape, lambda *_: (0,) * len(shape), pipeline_mode=pl.Buffered(1))


def _rms_scale(x):
    return lax.rsqrt(jnp.mean(x * x, axis=-1, keepdims=True) + EPS)


def _mix_in_kernel(x_ref, g_ref, w_ref, seg_ref, qg_ref, kg_ref, dww_ref, dwb_ref, lng_ref, lnb_ref,
                   q_ref, k_ref, v_ref, c_ref, cbuf_ref, *, attn_w, conv_w):
    tm = x_ref.shape[0]
    x = x_ref[...]
    u = (x * _rms_scale(x) * g_ref[...]).astype(BF16)

    def head_norm(cols, gain_ref, out_ref):
        t = jnp.dot(u, w_ref[:, cols:cols + attn_w], preferred_element_type=F32)
        ms = jnp.dot((t * t).astype(BF16), seg_ref[...], preferred_element_type=F32)
        out_ref[...] = (t * lax.rsqrt(ms + EPS) * gain_ref[...]).astype(BF16)

    head_norm(0, qg_ref, q_ref)
    head_norm(attn_w, kg_ref, k_ref)
    v_ref[...] = jnp.dot(u, w_ref[:, 2 * attn_w:3 * attn_w], preferred_element_type=F32).astype(BF16)

    ca = jnp.dot(u, w_ref[:, 3 * attn_w:3 * attn_w + conv_w], preferred_element_type=F32)
    cg = jnp.dot(u, w_ref[:, 3 * attn_w + conv_w:3 * attn_w + 2 * conv_w], preferred_element_type=F32)

    @pl.when(pl.program_id(1) == 0)
    def _():
        cbuf_ref[0:CONV_HALO, :] = jnp.zeros((CONV_HALO, conv_w), F32)

    cbuf_ref[CONV_HALO:CONV_HALO + tm, :] = ca * jax.nn.sigmoid(cg)

    first_tap = CONV_HALO - (CONV_KERNEL - 1)
    for r0 in range(0, tm, CONV_ROW_CHUNK):
        acc = jnp.broadcast_to(dwb_ref[...], (CONV_ROW_CHUNK, conv_w))
        for tap in range(CONV_KERNEL):
            acc = acc + dww_ref[tap:tap + 1, :] * cbuf_ref[r0 + first_tap + tap:r0 + first_tap + tap + CONV_ROW_CHUNK, :]
        mu = jnp.mean(acc, axis=-1, keepdims=True)
        xc = acc - mu
        y = xc * lax.rsqrt(jnp.mean(xc * xc, axis=-1, keepdims=True) + EPS)
        y = y * lng_ref[...] + lnb_ref[...]
        c_ref[r0:r0 + CONV_ROW_CHUNK, :] = (y * jax.nn.sigmoid(y)).astype(BF16)

    cbuf_ref[0:CONV_HALO, :] = cbuf_ref[tm:tm + CONV_HALO, :]


def _mix_in(x, g_mix, w_in, seg, q_gain, k_gain, dw_w, dw_b, ln_g, ln_b, *, attn_w, conv_w):
    bsz, seq, d = x.shape
    tm = min(ROW_TILE_IN, seq)
    assert seq % tm == 0 and tm % CONV_ROW_CHUNK == 0 and tm >= CONV_HALO
    row = lambda width: pl.BlockSpec((None, tm, width), lambda b, s: (b, s, 0))
    out_sds = jax.ShapeDtypeStruct((bsz, seq, attn_w), BF16)
    return pl.pallas_call(
        functools.partial(_mix_in_kernel, attn_w=attn_w, conv_w=conv_w),
        grid=(bsz, seq // tm),
        in_specs=[row(d), _const_spec(g_mix.shape), _const_spec(w_in.shape), _const_spec(seg.shape),
                  _const_spec(q_gain.shape), _const_spec(k_gain.shape), _const_spec(dw_w.shape),
                  _const_spec(dw_b.shape), _const_spec(ln_g.shape), _const_spec(ln_b.shape)],
        out_specs=[row(attn_w), row(attn_w), row(attn_w), row(conv_w)],
        out_shape=[out_sds, out_sds, out_sds, jax.ShapeDtypeStruct((bsz, seq, conv_w), BF16)],
        scratch_shapes=[pltpu.VMEM((tm + CONV_HALO, conv_w), F32)],
        compiler_params=pltpu.CompilerParams(
            dimension_semantics=("arbitrary", "arbitrary"), vmem_limit_bytes=VMEM_LIMIT_BYTES),
        name="mix_in",
    )(x, g_mix, w_in, seg, q_gain, k_gain, dw_w, dw_b, ln_g, ln_b)


def _stickbreak_kernel(q_ref, k_ref, v_ref, tri_ref, o_ref, q2_ref, z_ref, hilo_ref, acc_ref):
    bq, width = q_ref.shape
    n_pairs = width // LANES
    pairs = range(n_pairs)
    blk = pl.program_id(1)
    lanes_of = lambda pair: slice(pair * LANES, (pair + 1) * LANES)
    rows_of = lambda kv_blk: pl.ds(pl.multiple_of(jnp.maximum(kv_blk, 0) * bq, bq), bq)

    lane = lax.broadcasted_iota(jnp.int32, (bq, LANES), 1)
    zero = jnp.zeros((bq, LANES), q_ref.dtype)
    for pair in pairs:
        q = q_ref[:, lanes_of(pair)]
        q2_ref[pair, 0:bq, :] = jnp.where(lane < HEAD_DIM, q, zero)
        q2_ref[pair, bq:2 * bq, :] = jnp.where(lane < HEAD_DIM, zero, q)
        acc_ref[pair] = jnp.zeros(acc_ref.shape[1:], F32)

    def causal_mask():
        row = lax.broadcasted_iota(jnp.int32, (2 * bq, bq), 0)
        col = lax.broadcasted_iota(jnp.int32, (2 * bq, bq), 1)
        return col < jnp.where(row >= bq, row - bq, row)

    def logits(pair, kv_blk, diagonal=False):
        kb = k_ref[rows_of(kv_blk), lanes_of(pair)]
        z = lax.dot_general(q2_ref[pair], kb, (((1,), (1,)), ((), ())), preferred_element_type=F32)
        p = jnp.maximum(jnp.log(1.0 + jnp.exp2(jnp.minimum(z, SOFTPLUS_CLAMP))) * LOG2_E, z)
        if diagonal:
            p = jnp.where(causal_mask(), p, 0.0)
        hi = lax.bitcast_convert_type(lax.bitcast_convert_type(p, jnp.uint32) & jnp.uint32(0xFFFF0000), F32)
        z_ref[pair] = z
        hilo_ref[pair, :, 0:bq] = hi.astype(BF16)
        hilo_ref[pair, :, bq:2 * bq] = (p - hi).astype(BF16)

    def weights(pair, run, diagonal=False):
        neg_suffix = jnp.dot(hilo_ref[pair], tri_ref[...], preferred_element_type=F32)
        w = jnp.exp2(z_ref[pair] + neg_suffix + run)
        if diagonal:
            w = jnp.where(causal_mask(), w, 0.0)
        return w.astype(BF16), run + neg_suffix[:, 0:1]

    def values(pair, w, kv_blk):
        acc_ref[pair] += jnp.dot(w, v_ref[rows_of(kv_blk), lanes_of(pair)], preferred_element_type=F32)

    def step(kv_blk, runs, diagonal=False):
        ws, runs = zip(*(weights(pair, runs[pair], diagonal) for pair in pairs))
        for pair in pairs:
            logits(pair, kv_blk - 1)
        for pair in pairs:
            values(pair, ws[pair], kv_blk)
        return runs

    for pair in pairs:
        logits(pair, blk, diagonal=True)
    runs = step(blk, (jnp.zeros((2 * bq, 1), F32),) * n_pairs, diagonal=True)
    lax.fori_loop(0, blk, lambda t, runs: step(blk - 1 - t, runs), runs)
    for pair in pairs:
        acc = acc_ref[pair]
        o_ref[:, lanes_of(pair)] = jnp.where(lane < HEAD_DIM, acc[:bq], acc[bq:]).astype(o_ref.dtype)


def _stickbreak(q, k, v):
    bsz, seq, width = q.shape
    bq = min(ATTN_BLOCK, seq)
    assert seq % bq == 0 and width % LANES == 0
    n_pairs = width // LANES
    tri = -(lax.broadcasted_iota(jnp.int32, (bq, bq), 0) >= lax.broadcasted_iota(jnp.int32, (bq, bq), 1)).astype(BF16)
    tri = jnp.concatenate([tri, tri], axis=0)
    seq_spec = pl.BlockSpec((None, seq, width), lambda b, i: (b, 0, 0), pipeline_mode=pl.Buffered(1))
    blk_spec = pl.BlockSpec((None, bq, width), lambda b, i: (b, i, 0))
    return pl.pallas_call(
        _stickbreak_kernel,
        grid=(bsz, seq // bq),
        in_specs=[blk_spec, seq_spec, seq_spec, _const_spec(tri.shape)],
        out_specs=blk_spec,
        out_shape=jax.ShapeDtypeStruct((bsz, seq, width), BF16),
        compiler_params=pltpu.CompilerParams(
            dimension_semantics=("arbitrary", "arbitrary"), vmem_limit_bytes=VMEM_LIMIT_BYTES),
        scratch_shapes=[pltpu.VMEM((n_pairs, 2 * bq, LANES), BF16),
                        pltpu.VMEM((n_pairs, 2 * bq, bq), F32),
                        pltpu.VMEM((n_pairs, 2 * bq, 2 * bq), BF16),
                        pltpu.VMEM((n_pairs, 2 * bq, LANES), F32)],
        name="stickbreak",
    )(q, k, v, tri)


def _mix_out_ffn_ple_kernel(x_ref, a_ref, c_ref, p_ref, wo_ref, gf_ref, wu_ref, fw_ref, fb_ref, wd_ref,
                            gp_ref, wg_ref, wp_ref, o_ref, gbuf_ref, gcar_ref, *, d_ff, ff_chunk):
    tm = x_ref.shape[0]
    attn_w = a_ref.shape[1]

    h = (x_ref[...]
         + jnp.dot(a_ref[...], wo_ref[0:attn_w, :], preferred_element_type=F32)
         + jnp.dot(c_ref[...], wo_ref[attn_w:, :], preferred_element_type=F32))

    @pl.when(pl.program_id(1) == 0)
    def _():
        gcar_ref[...] = jnp.zeros(gcar_ref.shape, F32)

    u = (h * _rms_scale(h) * gf_ref[...]).astype(BF16)
    ffn = jnp.zeros_like(h)
    for c0 in range(0, d_ff, ff_chunk):
        gate = jnp.dot(u, wu_ref[:, c0:c0 + ff_chunk], preferred_element_type=F32)
        val = jnp.dot(u, wu_ref[:, d_ff + c0:d_ff + c0 + ff_chunk], preferred_element_type=F32)
        gbuf_ref[0:SUBLANES, :] = gcar_ref[:, c0:c0 + ff_chunk]
        gbuf_ref[SUBLANES:SUBLANES + tm, :] = gate
        gcar_ref[:, c0:c0 + ff_chunk] = gate[tm - SUBLANES:, :]
        conv = (fb_ref[:, c0:c0 + ff_chunk]
                + fw_ref[0:1, c0:c0 + ff_chunk] * gbuf_ref[SUBLANES - 2:SUBLANES - 2 + tm, :]
                + fw_ref[1:2, c0:c0 + ff_chunk] * gbuf_ref[SUBLANES - 1:SUBLANES - 1 + tm, :]
                + fw_ref[2:3, c0:c0 + ff_chunk] * gate)
        gelu = 0.5 * conv * (1.0 + lax.erf(conv * (2.0 ** -0.5)))
        act = (gelu * val).astype(BF16)
        ffn = ffn + jnp.dot(act, wd_ref[c0:c0 + ff_chunk, :], preferred_element_type=F32)
    h = h + ffn

    u = (h * _rms_scale(h) * gp_ref[...]).astype(BF16)
    ple_gate = jax.nn.sigmoid(jnp.dot(u, wg_ref[...], preferred_element_type=F32))
    ple = jnp.dot(p_ref[...].astype(BF16), wp_ref[...], preferred_element_type=F32)
    o_ref[...] = h + ple_gate * ple


def _mix_out_ffn_ple(x, attn, conv, p, w_out, g_ffn, w_up, fw, fb, w_down, g_ple, w_gate, w_proj):
    bsz, seq, d = x.shape
    d_ff = w_down.shape[0]
    tm = min(ROW_TILE_OUT, seq)
    ff_chunk = d_ff // 2
    assert seq % tm == 0 and ff_chunk % LANES == 0 and fw.shape[0] == FFN_CONV_KERNEL
    row = lambda width: pl.BlockSpec((None, tm, width), lambda b, s: (b, s, 0))
    consts = (w_out, g_ffn, w_up, fw, fb, w_down, g_ple, w_gate, w_proj)
    return pl.pallas_call(
        functools.partial(_mix_out_ffn_ple_kernel, d_ff=d_ff, ff_chunk=ff_chunk),
        grid=(bsz, seq // tm),
        in_specs=[row(d), row(attn.shape[2]), row(conv.shape[2]), row(p.shape[2])]
                 + [_const_spec(c.shape) for c in consts],
        out_specs=row(d),
        out_shape=jax.ShapeDtypeStruct((bsz, seq, d), F32),
        scratch_shapes=[pltpu.VMEM((tm + SUBLANES, ff_chunk), F32), pltpu.VMEM((SUBLANES, d_ff), F32)],
        compiler_params=pltpu.CompilerParams(
            dimension_semantics=("arbitrary", "arbitrary"), vmem_limit_bytes=VMEM_LIMIT_BYTES),
        name="mix_out_ffn_ple",
    )(x, attn, conv, p, *consts)


def kernel(x, p, g_mix, w_in, q_gain, k_gain, dw_w, dw_b, conv_ln_g, conv_ln_b, w_out,
           g_ffn, w_up, ffn_conv_w, ffn_conv_b, w_down, g_ple, w_ple_gate, w_ple_proj):
    depth = w_in.shape[0]
    conv_w = dw_w.shape[2]
    attn_w = (w_in.shape[2] - 2 * conv_w) // 3
    n_heads = attn_w // HEAD_DIM
    assert dw_w.shape[1] == CONV_KERNEL

    head_of = lax.broadcasted_iota(jnp.int32, (attn_w, attn_w), 0) // HEAD_DIM
    seg = jnp.where(head_of == head_of.T, 1.0 / HEAD_DIM, 0.0).astype(BF16)
    row2d = lambda a: a.reshape(1, -1)

    h = x
    for i in range(depth):
        qg = row2d(jnp.tile(q_gain[i], n_heads)) * (HEAD_DIM ** -0.5 * LOG2_E)
        kg = row2d(jnp.tile(k_gain[i], n_heads))
        q, k, v, conv = _mix_in(
            h, row2d(g_mix[i]), w_in[i].astype(BF16), seg, qg, kg, dw_w[i], row2d(dw_b[i]),
            row2d(conv_ln_g[i]), row2d(conv_ln_b[i]), attn_w=attn_w, conv_w=conv_w)
        attn = _stickbreak(q, k, v)
        h = _mix_out_ffn_ple(
            h, attn, conv, p[i], w_out[i].astype(BF16), row2d(g_ffn[i]), w_up[i].astype(BF16),
            ffn_conv_w[i], row2d(ffn_conv_b[i]), w_down[i].astype(BF16), row2d(g_ple[i]),
            w_ple_gate[i].astype(BF16), w_ple_proj[i].astype(BF16))
    return h
```

```python
import functools

import jax
import jax.numpy as jnp
from jax import lax
from jax.experimental import pallas as pl
from jax.experimental.pallas import tpu as pltpu

HEAD_DIM = 64
CONV_KERNEL = 31
FFN_CONV_KERNEL = 3
EPS = 1e-6

LANES = 128
SUBLANES = 8
VMEM_LIMIT_BYTES = 56 * 1024 * 1024

ROW_TILE_IN = 512
ROW_TILE_OUT = 512
ATTN_BLOCK = 256
CONV_HALO = 32
CONV_ROW_CHUNK = 64

LOG2_E = 1.4426950408889634
SOFTPLUS_CLAMP = 100.0

BF16 = jnp.bfloat16
F32 = jnp.float32


def _const_spec(shape):
    return pl.BlockSpec(shape, lambda *_: (0,) * len(shape), pipeline_mode=pl.Buffered(1))


def _rms_scale(x):
    return lax.rsqrt(jnp.mean(x * x, axis=-1, keepdims=True) + EPS)


def _mix_in_kernel(x_ref, g_ref, w_ref, seg_ref, qg_ref, kg_ref, dww_ref, dwb_ref, lng_ref, lnb_ref,
                   q_ref, k_ref, v_ref, c_ref, cbuf_ref, shift_ref, *, attn_w, conv_w):
    tm = x_ref.shape[0]
    x = x_ref[...]
    u = (x * _rms_scale(x) * g_ref[...]).astype(BF16)

    def head_norm(cols, gain_ref, out_ref):
        t = jnp.dot(u, w_ref[:, cols:cols + attn_w], preferred_element_type=F32)
        ms = jnp.dot((t * t).astype(BF16), seg_ref[...], preferred_element_type=F32)
        out_ref[...] = (t * lax.rsqrt(ms + EPS) * gain_ref[...]).astype(BF16)

    head_norm(0, qg_ref, q_ref)
    head_norm(attn_w, kg_ref, k_ref)
    v_ref[...] = jnp.dot(u, w_ref[:, 2 * attn_w:3 * attn_w], preferred_element_type=F32).astype(BF16)

    ca = jnp.dot(u, w_ref[:, 3 * attn_w:3 * attn_w + conv_w], preferred_element_type=F32)
    cg = jnp.dot(u, w_ref[:, 3 * attn_w + conv_w:3 * attn_w + 2 * conv_w], preferred_element_type=F32)

    @pl.when(pl.program_id(1) == 0)
    def _():
        cbuf_ref[0:CONV_HALO, :] = jnp.zeros((CONV_HALO, conv_w), F32)

    cbuf_ref[CONV_HALO:CONV_HALO + tm, :] = ca * jax.nn.sigmoid(cg)

    n_shift = shift_ref.shape[1]
    for r in range(1, SUBLANES):
        shift_ref[r - 1] = cbuf_ref[r:r + n_shift, :]

    first_tap = CONV_HALO - (CONV_KERNEL - 1)
    for r0 in range(0, tm, CONV_ROW_CHUNK):
        acc = jnp.broadcast_to(dwb_ref[...], (CONV_ROW_CHUNK, conv_w))
        for tap in range(CONV_KERNEL):
            lead, r = divmod(first_tap + tap, SUBLANES)
            rows = slice(r0 + lead * SUBLANES, r0 + lead * SUBLANES + CONV_ROW_CHUNK)
            window = cbuf_ref[rows, :] if r == 0 else shift_ref[r - 1, rows, :]
            acc = acc + dww_ref[tap:tap + 1, :] * window
        mu = jnp.mean(acc, axis=-1, keepdims=True)
        xc = acc - mu
        y = xc * lax.rsqrt(jnp.mean(xc * xc, axis=-1, keepdims=True) + EPS)
        y = y * lng_ref[...] + lnb_ref[...]
        c_ref[r0:r0 + CONV_ROW_CHUNK, :] = (y * jax.nn.sigmoid(y)).astype(BF16)

    cbuf_ref[0:CONV_HALO, :] = cbuf_ref[tm:tm + CONV_HALO, :]


def _mix_in(x, g_mix, w_in, seg, q_gain, k_gain, dw_w, dw_b, ln_g, ln_b, *, attn_w, conv_w):
    bsz, seq, d = x.shape
    tm = min(ROW_TILE_IN, seq)
    assert seq % tm == 0 and tm % CONV_ROW_CHUNK == 0 and tm >= CONV_HALO
    row = lambda width: pl.BlockSpec((None, tm, width), lambda b, s: (b, s, 0))
    out_sds = jax.ShapeDtypeStruct((bsz, seq, attn_w), BF16)
    return pl.pallas_call(
        functools.partial(_mix_in_kernel, attn_w=attn_w, conv_w=conv_w),
        grid=(bsz, seq // tm),
        in_specs=[row(d), _const_spec(g_mix.shape), _const_spec(w_in.shape), _const_spec(seg.shape),
                  _const_spec(q_gain.shape), _const_spec(k_gain.shape), _const_spec(dw_w.shape),
                  _const_spec(dw_b.shape), _const_spec(ln_g.shape), _const_spec(ln_b.shape)],
        out_specs=[row(attn_w), row(attn_w), row(attn_w), row(conv_w)],
        out_shape=[out_sds, out_sds, out_sds, jax.ShapeDtypeStruct((bsz, seq, conv_w), BF16)],
        scratch_shapes=[pltpu.VMEM((tm + CONV_HALO, conv_w), F32),
                        pltpu.VMEM((SUBLANES - 1, tm + CONV_HALO - SUBLANES, conv_w), F32)],
        compiler_params=pltpu.CompilerParams(
            dimension_semantics=("arbitrary", "arbitrary"), vmem_limit_bytes=VMEM_LIMIT_BYTES),
        name="mix_in",
    )(x, g_mix, w_in, seg, q_gain, k_gain, dw_w, dw_b, ln_g, ln_b)


def _stickbreak_kernel(q_ref, k_ref, v_ref, tri_ref, o_ref, q2_ref, acc_ref):
    bq, width = q_ref.shape
    n_pairs = width // LANES
    pairs = range(n_pairs)
    blk = pl.program_id(1)
    lanes_of = lambda pair: slice(pair * LANES, (pair + 1) * LANES)
    rows_of = lambda kv_blk: pl.ds(pl.multiple_of(jnp.maximum(kv_blk, 0) * bq, bq), bq)

    lane = lax.broadcasted_iota(jnp.int32, (bq, LANES), 1)
    zero = jnp.zeros((bq, LANES), q_ref.dtype)
    for pair in pairs:
        q = q_ref[:, lanes_of(pair)]
        q2_ref[pair, 0:bq, :] = jnp.where(lane < HEAD_DIM, q, zero)
        q2_ref[pair, bq:2 * bq, :] = jnp.where(lane < HEAD_DIM, zero, q)

    def causal_mask():
        row = lax.broadcasted_iota(jnp.int32, (2 * bq, bq), 0)
        col = lax.broadcasted_iota(jnp.int32, (2 * bq, bq), 1)
        return col < jnp.where(row >= bq, row - bq, row)

    def logits(pair, kv_blk, diagonal):
        kb = k_ref[rows_of(kv_blk), lanes_of(pair)]
        z = lax.dot_general(q2_ref[pair], kb, (((1,), (1,)), ((), ())), preferred_element_type=F32)
        p = jnp.maximum(jnp.log(1.0 + jnp.exp2(jnp.minimum(z, SOFTPLUS_CLAMP))) * LOG2_E, z)
        if diagonal:
            p = jnp.where(causal_mask(), p, 0.0)
        hi = lax.bitcast_convert_type(lax.bitcast_convert_type(p, jnp.uint32) & jnp.uint32(0xFFFF0000), F32)
        return z, jnp.concatenate([hi.astype(BF16), (p - hi).astype(BF16)], axis=1)

    def weights(z, hilo, diagonal):
        neg_suffix = jnp.dot(hilo, tri_ref[...], preferred_element_type=F32)
        w = jnp.exp2(z + neg_suffix)
        if diagonal:
            w = jnp.where(causal_mask(), w, 0.0)
        return w.astype(BF16), neg_suffix[:, 0:1]

    def values(pair, w, kv_blk):
        return jnp.dot(w, v_ref[rows_of(kv_blk), lanes_of(pair)], preferred_element_type=F32)

    def blocks(kv_blks, diagonals):
        zs = [[logits(pair, kv, diag) for pair in pairs] for kv, diag in zip(kv_blks, diagonals)]
        ws = [[weights(*zs[n][pair], diag) for pair in pairs] for n, diag in enumerate(diagonals)]
        pvs = [[values(pair, ws[n][pair][0], kv) for pair in pairs] for n, kv in enumerate(kv_blks)]
        totals = [[ws[n][pair][1] for pair in pairs] for n in range(len(kv_blks))]
        return pvs, totals

    pvs, totals = blocks([blk, blk - 1], [True, False])
    runs, scales = [], []
    for pair in pairs:
        run = jnp.broadcast_to(totals[0][pair], (2 * bq, LANES))
        scale = jnp.where(blk > 0, jnp.exp2(run), 0.0)
        acc_ref[pair] = pvs[0][pair] + scale * pvs[1][pair]
        run = run + totals[1][pair]
        runs.append(run)
        scales.append(jnp.exp2(run))

    def any_alive(scales):
        m = scales[0]
        for s in scales[1:]:
            m = jnp.maximum(m, s)
        return jnp.max(m) > 0.0

    def cond(carry):
        kv_blk, alive = carry[0], carry[1]
        return jnp.logical_and(kv_blk >= 0, alive)

    def body(carry):
        kv_blk, runs, scales = carry[0], carry[2], carry[3]
        pvs, totals = blocks([kv_blk], [False])
        new_runs, new_scales = [], []
        for pair in pairs:
            acc_ref[pair] += scales[pair] * pvs[0][pair]
            run = runs[pair] + totals[0][pair]
            new_runs.append(run)
            new_scales.append(jnp.exp2(run))
        return kv_blk - 1, any_alive(new_scales), tuple(new_runs), tuple(new_scales)

    lax.while_loop(cond, body, (blk - 2, any_alive(scales), tuple(runs), tuple(scales)))
    for pair in pairs:
        acc = acc_ref[pair]
        o_ref[:, lanes_of(pair)] = jnp.where(lane < HEAD_DIM, acc[:bq], acc[bq:]).astype(o_ref.dtype)


def _stickbreak(q, k, v):
    bsz, seq, width = q.shape
    bq = min(ATTN_BLOCK, seq)
    assert seq % bq == 0 and width % LANES == 0
    n_pairs = width // LANES
    tri = -(lax.broadcasted_iota(jnp.int32, (bq, bq), 0) >= lax.broadcasted_iota(jnp.int32, (bq, bq), 1)).astype(BF16)
    tri = jnp.concatenate([tri, tri], axis=0)
    seq_spec = pl.BlockSpec((None, seq, width), lambda b, i: (b, 0, 0), pipeline_mode=pl.Buffered(1))
    blk_spec = pl.BlockSpec((None, bq, width), lambda b, i: (b, i, 0))
    return pl.pallas_call(
        _stickbreak_kernel,
        grid=(bsz, seq // bq),
        in_specs=[blk_spec, seq_spec, seq_spec, _const_spec(tri.shape)],
        out_specs=blk_spec,
        out_shape=jax.ShapeDtypeStruct((bsz, seq, width), BF16),
        compiler_params=pltpu.CompilerParams(
            dimension_semantics=("arbitrary", "arbitrary"), vmem_limit_bytes=VMEM_LIMIT_BYTES),
        scratch_shapes=[pltpu.VMEM((n_pairs, 2 * bq, LANES), BF16),
                        pltpu.VMEM((n_pairs, 2 * bq, LANES), F32)],
        name="stickbreak",
    )(q, k, v, tri)


def _mix_out_ffn_ple_kernel(x_ref, a_ref, c_ref, p_ref, wo_ref, gf_ref, wu_ref, fw_ref, fb_ref, wd_ref,
                            gp_ref, wg_ref, wp_ref, o_ref, gbuf_ref, gcar_ref, *, d_ff, ff_chunk):
    tm = x_ref.shape[0]
    attn_w = a_ref.shape[1]

    h = (x_ref[...]
         + jnp.dot(a_ref[...], wo_ref[0:attn_w, :], preferred_element_type=F32)
         + jnp.dot(c_ref[...], wo_ref[attn_w:, :], preferred_element_type=F32))

    @pl.when(pl.program_id(1) == 0)
    def _():
        gcar_ref[...] = jnp.zeros(gcar_ref.shape, F32)

    u = (h * _rms_scale(h) * gf_ref[...]).astype(BF16)
    ffn = jnp.zeros_like(h)
    for c0 in range(0, d_ff, ff_chunk):
        gate = jnp.dot(u, wu_ref[:, c0:c0 + ff_chunk], preferred_element_type=F32)
        val = jnp.dot(u, wu_ref[:, d_ff + c0:d_ff + c0 + ff_chunk], preferred_element_type=F32)
        gbuf_ref[0:SUBLANES, :] = gcar_ref[:, c0:c0 + ff_chunk]
        gbuf_ref[SUBLANES:SUBLANES + tm, :] = gate
        gcar_ref[:, c0:c0 + ff_chunk] = gate[tm - SUBLANES:, :]
        conv = (fb_ref[:, c0:c0 + ff_chunk]
                + fw_ref[0:1, c0:c0 + ff_chunk] * gbuf_ref[SUBLANES - 2:SUBLANES - 2 + tm, :]
                + fw_ref[1:2, c0:c0 + ff_chunk] * gbuf_ref[SUBLANES - 1:SUBLANES - 1 + tm, :]
                + fw_ref[2:3, c0:c0 + ff_chunk] * gate)
        gelu = 0.5 * conv * (1.0 + lax.erf(conv * (2.0 ** -0.5)))
        act = (gelu * val).astype(BF16)
        ffn = ffn + jnp.dot(act, wd_ref[c0:c0 + ff_chunk, :], preferred_element_type=F32)
    h = h + ffn

    u = (h * _rms_scale(h) * gp_ref[...]).astype(BF16)
    ple_gate = jax.nn.sigmoid(jnp.dot(u, wg_ref[...], preferred_element_type=F32))
    ple = jnp.dot(p_ref[...].astype(BF16), wp_ref[...], preferred_element_type=F32)
    o_ref[...] = h + ple_gate * ple


def _mix_out_ffn_ple(x, attn, conv, p, w_out, g_ffn, w_up, fw, fb, w_down, g_ple, w_gate, w_proj):
    bsz, seq, d = x.shape
    d_ff = w_down.shape[0]
    tm = min(ROW_TILE_OUT, seq)
    ff_chunk = d_ff // 2
    assert seq % tm == 0 and ff_chunk % LANES == 0 and fw.shape[0] == FFN_CONV_KERNEL
    row = lambda width: pl.BlockSpec((None, tm, width), lambda b, s: (b, s, 0))
    consts = (w_out, g_ffn, w_up, fw, fb, w_down, g_ple, w_gate, w_proj)
    return pl.pallas_call(
        functools.partial(_mix_out_ffn_ple_kernel, d_ff=d_ff, ff_chunk=ff_chunk),
        grid=(bsz, seq // tm),
        in_specs=[row(d), row(attn.shape[2]), row(conv.shape[2]), row(p.shape[2])]
                 + [_const_spec(c.shape) for c in consts],
        out_specs=row(d),
        out_shape=jax.ShapeDtypeStruct((bsz, seq, d), F32),
        scratch_shapes=[pltpu.VMEM((tm + SUBLANES, ff_chunk), F32), pltpu.VMEM((SUBLANES, d_ff), F32)],
        compiler_params=pltpu.CompilerParams(
            dimension_semantics=("arbitrary", "arbitrary"), vmem_limit_bytes=VMEM_LIMIT_BYTES),
        name="mix_out_ffn_ple",
    )(x, attn, conv, p, *consts)


def kernel(x, p, g_mix, w_in, q_gain, k_gain, dw_w, dw_b, conv_ln_g, conv_ln_b, w_out,
           g_ffn, w_up, ffn_conv_w, ffn_conv_b, w_down, g_ple, w_ple_gate, w_ple_proj):
    depth = w_in.shape[0]
    conv_w = dw_w.shape[2]
    attn_w = (w_in.shape[2] - 2 * conv_w) // 3
    n_heads = attn_w // HEAD_DIM
    assert dw_w.shape[1] == CONV_KERNEL

    head_of = lax.broadcasted_iota(jnp.int32, (attn_w, attn_w), 0) // HEAD_DIM
    seg = jnp.where(head_of == head_of.T, 1.0 / HEAD_DIM, 0.0).astype(BF16)
    row2d = lambda a: a.reshape(1, -1)

    h = x
    for i in range(depth):
        qg = row2d(jnp.tile(q_gain[i], n_heads)) * (HEAD_DIM ** -0.5 * LOG2_E)
        kg = row2d(jnp.tile(k_gain[i], n_heads))
        q, k, v, conv = _mix_in(
            h, row2d(g_mix[i]), w_in[i].astype(BF16), seg, qg, kg, dw_w[i], row2d(dw_b[i]),
            row2d(conv_ln_g[i]), row2d(conv_ln_b[i]), attn_w=attn_w, conv_w=conv_w)
        attn = _stickbreak(q, k, v)
        h = _mix_out_ffn_ple(
            h, attn, conv, p[i], w_out[i].astype(BF16), row2d(g_ffn[i]), w_up[i].astype(BF16),
            ffn_conv_w[i], row2d(ffn_conv_b[i]), w_down[i].astype(BF16), row2d(g_ple[i]),
            w_ple_gate[i].astype(BF16), w_ple_proj[i].astype(BF16))
    return h
```
